```python
import math
import jax
import jax.numpy as jnp
from jax import lax
import numpy as np

D_MODEL = 4096
BATCH = 1
SEQ = 8192
DEPTH = 1

HEAD_DIM = 128
MIX_WIDTH = D_MODEL
ATTN_HEADS = MIX_WIDTH // 2 // HEAD_DIM
ATTN_WIDTH = ATTN_HEADS * HEAD_DIM
CONV_WIDTH = MIX_WIDTH - ATTN_WIDTH
CONV_GROUPS = CONV_WIDTH // HEAD_DIM
CONV_KERNEL = 31
Q_BLOCK = 128
FORGET_BIAS_MEAN = 2.0
IN_WIDTH = 3 * ATTN_WIDTH + ATTN_HEADS + 2 * CONV_WIDTH

N_EXPERTS = 64
TOP_K = 6
EXPERT_FF = D_MODEL // 8
SHARED_FF = EXPERT_FF
N_GROUPS = 8
TOPK_GROUPS = 4
ROUTE_SCALE = 2.5
MOE_BLOCK = 128

LN_EPS = 1e-5

kernel_name = "hybrid_fox_conformer_moe_deepnorm_adaln"


def layer_norm(x, g, b):
    xf = x.astype(jnp.float32)
    mu = jnp.mean(xf, axis=-1, keepdims=True)
    var = jnp.mean(jnp.square(xf - mu), axis=-1, keepdims=True)
    return ((xf - mu) * lax.rsqrt(var + LN_EPS)).astype(x.dtype) * g + b


def forgetting_attention(q, k, v, log_f):
    B, S, H, Dh = q.shape
    scale = Dh ** -0.5
    cum = jnp.transpose(jnp.cumsum(log_f, axis=1), (0, 2, 1))
    k_pos = jnp.arange(S)

    def block(i):
        start = i * Q_BLOCK
        qb = lax.dynamic_slice_in_dim(q, start, Q_BLOCK, axis=1)
        cq = lax.dynamic_slice_in_dim(cum, start, Q_BLOCK, axis=2)
        s = jnp.einsum('bqhd,bkhd->bhqk', qb, k).astype(jnp.float32) * scale
        s = s + cq[..., None] - cum[:, :, None, :]
        q_pos = start + jnp.arange(Q_BLOCK)
        s = jnp.where(k_pos[None, :] <= q_pos[:, None], s, -jnp.inf)
        p = jax.nn.softmax(s, axis=-1)
        return jnp.einsum('bhqk,bkhd->bqhd', p.astype(v.dtype), v)

    out = lax.map(block, jnp.arange(S // Q_BLOCK))
    return jnp.moveaxis(out, 0, 1).reshape(B, S, H * Dh)


def conformer_conv(val, gate, w_dw, b_dw, norm_g, norm_b):
    B, S, C = val.shape
    u = val * jax.nn.sigmoid(gate)
    u = lax.conv_general_dilated(
        u, w_dw[:, None, :], window_strides=(1,), padding=[(CONV_KERNEL - 1, 0)],
        dimension_numbers=('NWC', 'WIO', 'NWC'), feature_group_count=C) + b_dw
    ug = u.reshape(B, S, CONV_GROUPS, C // CONV_GROUPS).astype(jnp.float32)
    mu = jnp.mean(ug, axis=-1, keepdims=True)
    var = jnp.mean(jnp.square(ug - mu), axis=-1, keepdims=True)
    u = ((ug - mu) * lax.rsqrt(var + LN_EPS)).reshape(B, S, C).astype(val.dtype) * norm_g + norm_b
    return jax.nn.silu(u)


def moe_ffn(h, w_router, router_bias, w_exp_gate, w_exp_up, w_exp_down, w_sh_gate, w_sh_up, w_sh_down):
    B, S, D = h.shape
    T = B * S
    xf = h.reshape(T, D)
    scores = jax.nn.sigmoid((xf @ w_router).astype(jnp.float32))
    biased = scores + router_bias.astype(jnp.float32)
    per_group = N_EXPERTS // N_GROUPS
    grp_score = lax.top_k(biased.reshape(T, N_GROUPS, per_group), 2)[0].sum(-1)
    _, grp_idx = lax.top_k(grp_score, TOPK_GROUPS)
    grp_mask = jax.nn.one_hot(grp_idx, N_GROUPS, dtype=jnp.float32).sum(1) > 0
    exp_mask = jnp.repeat(grp_mask, per_group, axis=1)
    _, top_idx = lax.top_k(jnp.where(exp_mask, biased, -jnp.inf), TOP_K)
    top_w = jnp.take_along_axis(scores, top_idx, axis=1)
    top_w = top_w / jnp.sum(top_w, axis=-1, keepdims=True) * ROUTE_SCALE

    TK = T * TOP_K
    expert_flat = top_idx.reshape(TK).astype(jnp.int32)
    token_flat = jnp.repeat(jnp.arange(T, dtype=jnp.int32), TOP_K)
    gate_flat = top_w.reshape(TK).astype(xf.dtype)
    order = jnp.argsort(expert_flat)
    sorted_e = expert_flat[order]
    sorted_tok = token_flat[order]
    sorted_g = gate_flat[order]
    counts = jnp.bincount(expert_flat, length=N_EXPERTS).astype(jnp.int32)
    group_start = jnp.cumsum(counts) - counts
    padded = (counts + MOE_BLOCK - 1) // MOE_BLOCK * MOE_BLOCK
    padded_end = jnp.cumsum(padded)
    padded_start = padded_end - padded
    dest = padded_start[sorted_e] + (jnp.arange(TK, dtype=jnp.int32) - group_start[sorted_e])
    n_blocks = (TK + N_EXPERTS * (MOE_BLOCK - 1) + MOE_BLOCK - 1) // MOE_BLOCK
    n_rows = n_blocks * MOE_BLOCK
    row_tok = jnp.zeros((n_rows,), jnp.int32).at[dest].set(sorted_tok)
    row_gate = jnp.zeros((n_rows,), xf.dtype).at[dest].set(sorted_g)
    block_start = jnp.arange(n_blocks, dtype=jnp.int32) * MOE_BLOCK
    block_e = jnp.clip(jnp.searchsorted(padded_end, block_start, side='right'), 0, N_EXPERTS - 1)

    def body(acc, blk):
        tok, g, e = blk
        xb = xf[tok]
        hid = jax.nn.silu(xb @ w_exp_gate[e]) * (xb @ w_exp_up[e])
        y = hid @ w_exp_down[e]
        return acc.at[tok].add(y * g[:, None]), None

    routed, _ = lax.scan(body, jnp.zeros_like(xf),
                         (row_tok.reshape(n_blocks, MOE_BLOCK), row_gate.reshape(n_blocks, MOE_BLOCK), block_e))
    shared = (jax.nn.silu(xf @ w_sh_gate) * (xf @ w_sh_up)) @ w_sh_down
    return (routed + shared).reshape(B, S, D)


def setup_inputs(seed: int = 0) -> dict:
    key = jax.random.key(seed)
    ks = jax.random.split(key, 23)
    L, D, E, F = DEPTH, D_MODEL, N_EXPERTS, EXPERT_FF
    beta = (8.0 * DEPTH) ** -0.25
    nrm = jax.random.normal
    return {
        'x': nrm(ks[0], (BATCH, SEQ, D)),
        'c': nrm(ks[1], (BATCH, D)),
        'w_ada': nrm(ks[2], (L, D, 6 * D)) * (0.5 * D ** -0.5),
        'b_ada': 0.02 * nrm(ks[3], (L, 6 * D)),
        'w_in': nrm(ks[4], (L, D, IN_WIDTH)) * D ** -0.5,
        'b_forget': FORGET_BIAS_MEAN + 0.1 * nrm(ks[5], (L, ATTN_HEADS)),
        'w_dw': nrm(ks[6], (L, CONV_KERNEL, CONV_WIDTH)) * CONV_KERNEL ** -0.5,
        'b_dw': 0.02 * nrm(ks[7], (L, CONV_WIDTH)),
        'conv_norm_g': 1.0 + 0.02 * nrm(ks[8], (L, CONV_WIDTH)),
        'conv_norm_b': 0.02 * nrm(ks[9], (L, CONV_WIDTH)),
        'w_out': nrm(ks[10], (L, MIX_WIDTH, D)) * (MIX_WIDTH ** -0.5 * beta),
        'ln1_g': 1.0 + 0.02 * nrm(ks[11], (L, D)),
        'ln1_b': 0.02 * nrm(ks[12], (L, D)),
        'w_router': nrm(ks[13], (L, D, E)) * D ** -0.5,
        'router_bias': 0.01 * nrm(ks[14], (L, E)),
        'w_exp_gate': nrm(ks[15], (L, E, D, F)) * D ** -0.5,
        'w_exp_up': nrm(ks[16], (L, E, D, F)) * D ** -0.5,
        'w_exp_down': nrm(ks[17], (L, E, F, D)) * (F ** -0.5 * beta),
        'w_sh_gate': nrm(ks[18], (L, D, SHARED_FF)) * D ** -0.5,
        'w_sh_up': nrm(ks[19], (L, D, SHARED_FF)) * D ** -0.5,
        'w_sh_down': nrm(ks[20], (L, SHARED_FF, D)) * (SHARED_FF ** -0.5 * beta),
        'ln2_g': 1.0 + 0.02 * nrm(ks[21], (L, D)),
        'ln2_b': 0.02 * nrm(ks[22], (L, D)),
    }


def reference(x, c, w_ada, b_ada, w_in, b_forget, w_dw, b_dw, conv_norm_g, conv_norm_b, w_out,
              ln1_g, ln1_b, w_router, router_bias, w_exp_gate, w_exp_up, w_exp_down,
              w_sh_gate, w_sh_up, w_sh_down, ln2_g, ln2_b):
    B, S, D = x.shape
    alpha = (2.0 * DEPTH) ** 0.25
    splits = [ATTN_WIDTH, 2 * ATTN_WIDTH, 3 * ATTN_WIDTH, 3 * ATTN_WIDTH + ATTN_HEADS,
              3 * ATTN_WIDTH + ATTN_HEADS + CONV_WIDTH]
    for l in range(DEPTH):
        mod = (jax.nn.silu(c) @ w_ada[l] + b_ada[l])[:, None, :]
        shift_a, scale_a, gate_a, shift_f, scale_f, gate_f = jnp.split(mod, 6, axis=-1)

        h = x * (1.0 + scale_a) + shift_a
        proj = h @ w_in[l]
        q, k, v, f_logit, g_val, g_gate = jnp.split(proj, splits, axis=-1)
        log_f = jax.nn.log_sigmoid((f_logit + b_forget[l]).astype(jnp.float32))
        attn = forgetting_attention(q.reshape(B, S, ATTN_HEADS, HEAD_DIM),
                                    k.reshape(B, S, ATTN_HEADS, HEAD_DIM),
                                    v.reshape(B, S, ATTN_HEADS, HEAD_DIM), log_f)
        conv = conformer_conv(g_val, g_gate, w_dw[l], b_dw[l], conv_norm_g[l], conv_norm_b[l])
        mix = jnp.concatenate([attn, conv], axis=-1) @ w_out[l]
        x = layer_norm(alpha * x + gate_a * mix, ln1_g[l], ln1_b[l])

        h2 = x * (1.0 + scale_f) + shift_f
        ffn = moe_ffn(h2, w_router[l], router_bias[l], w_exp_gate[l], w_exp_up[l], w_exp_down[l],
                      w_sh_gate[l], w_sh_up[l], w_sh_down[l])
        x = layer_norm(alpha * x + gate_f * ffn, ln2_g[l], ln2_b[l])
    return x
```

```python
import functools

import jax
import jax.numpy as jnp
from jax import lax
from jax.experimental import pallas as pl
from jax.experimental.pallas import tpu as pltpu

HEAD_DIM = 128
CONV_KERNEL = 31
CONV_HALO = 32
N_GROUPS = 8
TOPK_GROUPS = 4
TOP_K = 6
TOP_K_PAD = 8
ROUTE_SCALE = 2.5
LN_EPS = 1e-5
LANES = 128
SUBLANES = 8
MOE_ROW_BLOCK = 256
VMEM_LIMIT = 56 * 1024 * 1024

_F32 = jnp.float32
_BF16 = jnp.bfloat16


def _params(semantics, vmem=VMEM_LIMIT):
    return pltpu.CompilerParams(dimension_semantics=semantics, vmem_limit_bytes=vmem)


def _silu(v):
    return v * jax.nn.sigmoid(v)


def _adaln_kernel(c_ref, w_ref, b_ref, o_ref, *, chunk):
    d, tn = w_ref.shape

    def body(i, acc):
        r0 = pl.multiple_of(i * chunk, chunk)
        cs = c_ref[pl.ds(r0, chunk), :]
        cs = _silu(cs)
        prod = w_ref[pl.ds(r0, chunk), :] * cs
        return acc + jnp.sum(prod.reshape(chunk // SUBLANES, SUBLANES, tn), axis=0)

    acc = lax.fori_loop(0, d // chunk, body, jnp.zeros((SUBLANES, tn), _F32))
    o_ref[...] = jnp.sum(acc, axis=0, keepdims=True) + b_ref[...]


def _adaln(c_col, w_ada, b_ada, tn=512, chunk=256):
    d, n = w_ada.shape
    tn = min(tn, n)
    chunk = min(chunk, d)
    return pl.pallas_call(
        functools.partial(_adaln_kernel, chunk=chunk),
        grid=(n // tn,),
        in_specs=[pl.BlockSpec((d, 1), lambda j: (0, 0)),
                  pl.BlockSpec((d, tn), lambda j: (0, j)),
                  pl.BlockSpec((1, tn), lambda j: (0, j))],
        out_specs=pl.BlockSpec((1, tn), lambda j: (0, j)),
        out_shape=jax.ShapeDtypeStruct((1, n), _F32),
        compiler_params=_params(("parallel",)),
        name="adaln",
    )(c_col, w_ada, b_ada)


def _modulate_kernel(x_ref, sc_ref, sh_ref, o_ref):
    o_ref[...] = (x_ref[...] * (1.0 + sc_ref[...]) + sh_ref[...]).astype(o_ref.dtype)


def _modulate(x, scale, shift, tm=512):
    t, d = x.shape
    tm = min(tm, t)
    return pl.pallas_call(
        _modulate_kernel,
        grid=(t // tm,),
        in_specs=[pl.BlockSpec((tm, d), lambda i: (i, 0)),
                  pl.BlockSpec((1, d), lambda i: (0, 0)),
                  pl.BlockSpec((1, d), lambda i: (0, 0))],
        out_specs=pl.BlockSpec((tm, d), lambda i: (i, 0)),
        out_shape=jax.ShapeDtypeStruct((t, d), _BF16),
        compiler_params=_params(("parallel",)),
        name="modulate",
    )(x, scale, shift)


def _matmul_kernel(a_ref, w_ref, o_ref):
    o_ref[...] = jnp.dot(a_ref[...], w_ref[...], preferred_element_type=_F32).astype(o_ref.dtype)


def _matmul(a, w, out_dtype, tm=512, tn=1024, name="matmul"):
    m, k = a.shape
    _, n = w.shape
    tm, tn = min(tm, m), min(tn, n)
    return pl.pallas_call(
        _matmul_kernel,
        grid=(m // tm, n // tn),
        in_specs=[pl.BlockSpec((tm, k), lambda i, j: (i, 0)),
                  pl.BlockSpec((k, tn), lambda i, j: (0, j))],
        out_specs=pl.BlockSpec((tm, tn), lambda i, j: (i, j)),
        out_shape=jax.ShapeDtypeStruct((m, n), out_dtype),
        compiler_params=_params(("parallel", "parallel")),
        name=name,
    )(a, w)


def _split3(v):
    hi = v.astype(_BF16)
    r1 = v - hi.astype(_F32)
    mid = r1.astype(_BF16)
    lo = (r1 - mid.astype(_F32)).astype(_BF16)
    return hi, mid, lo


def _forget_kernel(f_ref, b_ref, tri_ref, qx_ref, kx_ref, carry_ref, *, heads):
    i = pl.program_id(0)

    @pl.when(i == 0)
    def _():
        carry_ref[...] = jnp.zeros_like(carry_ref)

    z = f_ref[...] + b_ref[...]
    logf = jnp.minimum(z, 0.0) - jnp.log1p(jnp.exp(-jnp.abs(z)))
    hi, mid, lo = _split3(logf)
    tri = tri_ref[...]
    cum = (jnp.dot(tri, hi, preferred_element_type=_F32)
           + jnp.dot(tri, mid, preferred_element_type=_F32)
           + jnp.dot(tri, lo, preferred_element_type=_F32)) + carry_ref[...]
    tb = cum.shape[0]
    carry_ref[...] = cum[tb - 1:tb, :]

    lane = lax.broadcasted_iota(jnp.int32, (tb, LANES), 1)
    one = jnp.ones((tb, LANES), _F32)
    zero = jnp.zeros((tb, LANES), _F32)
    for h in range(heads):
        c = jnp.broadcast_to(cum[:, h:h + 1], (tb, LANES))
        ch, cm, cl = (p.astype(_F32) for p in _split3(c))
        qx = jnp.where(lane == 0, ch, jnp.where(lane == 1, cm, jnp.where(lane == 2, cl,
                       jnp.where(lane < 6, one, zero))))
        kx = jnp.where(lane < 3, one, jnp.where(lane == 3, -ch, jnp.where(lane == 4, -cm,
                       jnp.where(lane == 5, -cl, zero))))
        qx_ref[h] = qx.astype(_BF16)
        kx_ref[h] = kx.astype(_BF16)


def _forget_cum(f_logit, b_pad, heads, tb=512):
    t = f_logit.shape[0]
    tb = min(tb, t)
    tri = (lax.broadcasted_iota(jnp.int32, (tb, tb), 0)
           >= lax.broadcasted_iota(jnp.int32, (tb, tb), 1)).astype(_BF16)
    shape = jax.ShapeDtypeStruct((heads, t, LANES), _BF16)
    return pl.pallas_call(
        functools.partial(_forget_kernel, heads=heads),
        grid=(t // tb,),
        in_specs=[pl.BlockSpec((tb, LANES), lambda i: (i, 0)),
                  pl.BlockSpec((1, LANES), lambda i: (0, 0)),
                  pl.BlockSpec((tb, tb), lambda i: (0, 0))],
        out_specs=[pl.BlockSpec((heads, tb, LANES), lambda i: (0, i, 0)),
                   pl.BlockSpec((heads, tb, LANES), lambda i: (0, i, 0))],
        out_shape=[shape, shape],
        scratch_shapes=[pltpu.VMEM((1, LANES), _F32)],
        compiler_params=_params(("arbitrary",)),
        name="forget_cum",
    )(f_logit, b_pad, tri)


def _attn_kernel(q_ref, qx_ref, k_ref, kx_ref, v_ref, o_ref, m_ref, l_ref, acc_ref, *, tk):
    i = pl.program_id(1)
    tq = q_ref.shape[0]
    qq = jnp.concatenate([q_ref[...], qx_ref[...]], axis=1)
    m_ref[...] = jnp.full_like(m_ref, -jnp.inf)
    l_ref[...] = jnp.zeros_like(l_ref)
    acc_ref[...] = jnp.zeros_like(acc_ref)

    def step(j, masked):
        k0 = pl.multiple_of(j * tk, tk)
        kk = jnp.concatenate([k_ref[pl.ds(k0, tk), :], kx_ref[pl.ds(k0, tk), :]], axis=1)
        s = lax.dot_general(qq, kk, (((1,), (1,)), ((), ())), preferred_element_type=_F32)
        if masked:
            row = lax.broadcasted_iota(jnp.int32, (tq, tk), 0) + i * tq
            col = lax.broadcasted_iota(jnp.int32, (tq, tk), 1) + k0
            s = jnp.where(col <= row, s, -jnp.inf)
        m_old = m_ref[...]
        m_new = jnp.maximum(m_old, jnp.max(s, axis=1, keepdims=True))
        alpha = jnp.exp(m_old - m_new)
        p = jnp.exp(s - m_new)
        l_ref[...] = alpha * l_ref[...] + jnp.sum(p, axis=1, keepdims=True)
        acc_ref[...] = alpha * acc_ref[...] + jnp.dot(
            p.astype(_BF16), v_ref[pl.ds(k0, tk), :], preferred_element_type=_F32)
        m_ref[...] = m_new

    n_full = (i * tq) // tk

    def full_body(j, c):
        step(j, False)
        return c

    lax.fori_loop(0, n_full, full_body, 0)
    for d in range(tq // tk):
        step(n_full + d, True)
    o_ref[...] = (acc_ref[...] / l_ref[...]).astype(o_ref.dtype)


def _attention(qkv, qx, kx, heads, tq=512, tk=512):
    t = qkv.shape[0]
    tq, tk = min(tq, t), min(tk, t)
    return pl.pallas_call(
        functools.partial(_attn_kernel, tk=tk),
        grid=(heads, t // tq),
        in_specs=[pl.BlockSpec((tq, HEAD_DIM), lambda h, i: (i, h)),
                  pl.BlockSpec((None, tq, LANES), lambda h, i: (h, i, 0)),
                  pl.BlockSpec((t, HEAD_DIM), lambda h, i: (0, heads + h)),
                  pl.BlockSpec((None, t, LANES), lambda h, i: (h, 0, 0)),
                  pl.BlockSpec((t, HEAD_DIM), lambda h, i: (0, 2 * heads + h))],
        out_specs=pl.BlockSpec((tq, HEAD_DIM), lambda h, i: (i, h)),
        out_shape=jax.ShapeDtypeStruct((t, heads * HEAD_DIM), _BF16),
        scratch_shapes=[pltpu.VMEM((tq, 1), _F32), pltpu.VMEM((tq, 1), _F32),
                        pltpu.VMEM((tq, HEAD_DIM), _F32)],
        compiler_params=_params(("parallel", "parallel")),
        name="attention",
    )(qkv, qx, qkv, kx, qkv)


def _conv_kernel(val_ref, gate_ref, hval_ref, hgate_ref, w_ref, b_ref, g_ref, nb_ref, o_ref,
                 ue_ref, us_ref, *, chunk):
    i = pl.program_id(1)
    tt, tc = val_ref.shape
    u_halo = hval_ref[...] * jax.nn.sigmoid(hgate_ref[...])
    ue_ref[0:CONV_HALO, :] = jnp.where(i > 0, u_halo, 0.0)
    ue_ref[CONV_HALO:CONV_HALO + tt, :] = val_ref[...] * jax.nn.sigmoid(gate_ref[...])
    span = tt + CONV_HALO - SUBLANES
    for r in range(1, SUBLANES):
        us_ref[r - 1, 0:span, :] = ue_ref[r:r + span, :]

    first_tap = CONV_HALO - (CONV_KERNEL - 1)
    bias = b_ref[...]
    gain = g_ref[...]
    nbias = nb_ref[...]

    def body(c, carry):
        r0 = pl.multiple_of(c * chunk, chunk)
        acc = jnp.broadcast_to(bias, (chunk, tc))
        for k in range(CONV_KERNEL):
            a, r = divmod(k + first_tap, SUBLANES)
            rows = pl.ds(r0 + a * SUBLANES, chunk)
            shifted = ue_ref[rows, :] if r == 0 else us_ref[r - 1, rows, :]
            acc = acc + w_ref[k:k + 1, :] * shifted
        outs = []
        for g in range(tc // HEAD_DIM):
            ug = acc[:, g * HEAD_DIM:(g + 1) * HEAD_DIM]
            mu = jnp.mean(ug, axis=1, keepdims=True)
            cen = ug - mu
            var = jnp.mean(cen * cen, axis=1, keepdims=True)
            outs.append(cen * lax.rsqrt(var + LN_EPS))
        y = jnp.concatenate(outs, axis=1) if len(outs) > 1 else outs[0]
        y = y * gain + nbias
        o_ref[pl.ds(r0, chunk), :] = _silu(y).astype(o_ref.dtype)
        return carry

    lax.fori_loop(0, tt // chunk, body, 0)


def _conv(conv_in, w_dw, b_dw, norm_g, norm_b, tt=1024, tc=256, chunk=64):
    t, two_c = conv_in.shape
    cw = two_c // 2
    tt, tc, chunk = min(tt, t), min(tc, cw), min(chunk, t)
    nc = cw // tc
    hb = tt // CONV_HALO
    w_pad = jnp.pad(w_dw, ((0, CONV_HALO - CONV_KERNEL), (0, 0)))
    halo_idx = lambda c, i: (jnp.maximum(i * hb - 1, 0), c)
    halo_idx_gate = lambda c, i: (jnp.maximum(i * hb - 1, 0), nc + c)
    return pl.pallas_call(
        functools.partial(_conv_kernel, chunk=chunk),
        grid=(nc, t // tt),
        in_specs=[pl.BlockSpec((tt, tc), lambda c, i: (i, c)),
                  pl.BlockSpec((tt, tc), lambda c, i: (i, nc + c)),
                  pl.BlockSpec((CONV_HALO, tc), halo_idx),
                  pl.BlockSpec((CONV_HALO, tc), halo_idx_gate),
                  pl.BlockSpec((CONV_HALO, tc), lambda c, i: (0, c)),
                  pl.BlockSpec((1, tc), lambda c, i: (0, c)),
                  pl.BlockSpec((1, tc), lambda c, i: (0, c)),
                  pl.BlockSpec((1, tc), lambda c, i: (0, c))],
        out_specs=pl.BlockSpec((tt, tc), lambda c, i: (i, c)),
        out_shape=jax.ShapeDtypeStruct((t, cw), _BF16),
        scratch_shapes=[pltpu.VMEM((tt + CONV_HALO, tc), _F32),
                        pltpu.VMEM((SUBLANES - 1, tt + CONV_HALO, tc), _F32)],
        compiler_params=_params(("parallel", "parallel")),
        name="conv",
    )(conv_in, conv_in, conv_in, conv_in, w_pad, b_dw, norm_g, norm_b)


def _outproj_kernel(a_ref, c_ref, wa_ref, wc_ref, x_ref, gate_ref, o_ref, *, alpha):
    mix = (jnp.dot(a_ref[...], wa_ref[...], preferred_element_type=_F32)
           + jnp.dot(c_ref[...], wc_ref[...], preferred_element_type=_F32))
    o_ref[...] = alpha * x_ref[...] + gate_ref[...] * mix


def _outproj(attn, conv, w_out, x, gate, alpha, tm=512, tn=1024):
    t, ka = attn.shape
    kc = conv.shape[1]
    d = w_out.shape[1]
    assert ka == kc
    tm, tn = min(tm, t), min(tn, d)
    return pl.pallas_call(
        functools.partial(_outproj_kernel, alpha=alpha),
        grid=(t // tm, d // tn),
        in_specs=[pl.BlockSpec((tm, ka), lambda i, j: (i, 0)),
                  pl.BlockSpec((tm, kc), lambda i, j: (i, 0)),
                  pl.BlockSpec((ka, tn), lambda i, j: (0, j)),
                  pl.BlockSpec((kc, tn), lambda i, j: (1, j)),
                  pl.BlockSpec((tm, tn), lambda i, j: (i, j)),
                  pl.BlockSpec((1, tn), lambda i, j: (0, j))],
        out_specs=pl.BlockSpec((tm, tn), lambda i, j: (i, j)),
        out_shape=jax.ShapeDtypeStruct((t, d), _F32),
        compiler_params=_params(("parallel", "parallel")),
        name="outproj",
    )(attn, conv, w_out, w_out, x, gate)


def _layer_norm_rows(y, g, b):
    mu = jnp.mean(y, axis=1, keepdims=True)
    cen = y - mu
    var = jnp.mean(cen * cen, axis=1, keepdims=True)
    return cen * lax.rsqrt(var + LN_EPS) * g + b


def _pack_bf16_pairs(h):
    n = h.shape[1] // 2
    bits = pltpu.bitcast(h.astype(_BF16).astype(_F32), jnp.uint32)
    return (bits[:, :n] & jnp.uint32(0xFFFF0000)) | (bits[:, n:] >> 16)


def _unpack_bf16_pairs(p):
    hi = pltpu.bitcast(p & jnp.uint32(0xFFFF0000), _F32)
    lo = pltpu.bitcast(p << 16, _F32)
    return jnp.concatenate([hi, lo], axis=1)


def _load_token_rows(ref, rows, sub):
    return jnp.concatenate([ref[pl.ds(j, rows, stride=sub), :] for j in range(sub)], axis=1)


def _store_token_rows(ref, words, sub):
    rows = words.shape[0]
    for j in range(sub):
        ref[pl.ds(j, rows, stride=sub), :] = words[:, j * LANES:(j + 1) * LANES]


def _ln_mod_kernel(y_ref, g_ref, b_ref, sc_ref, sh_ref, x1_ref, hp_ref, *, sub):
    xn = _layer_norm_rows(y_ref[...], g_ref[...], b_ref[...])
    x1_ref[...] = xn
    _store_token_rows(hp_ref, _pack_bf16_pairs(xn * (1.0 + sc_ref[...]) + sh_ref[...]), sub)


def _ln_mod(y, g, b, scale, shift, tm=256):
    t, d = y.shape
    tm = min(tm, t)
    sub = d // 2 // LANES
    vec = pl.BlockSpec((1, d), lambda i: (0, 0))
    return pl.pallas_call(
        functools.partial(_ln_mod_kernel, sub=sub),
        grid=(t // tm,),
        in_specs=[pl.BlockSpec((tm, d), lambda i: (i, 0)), vec, vec, vec, vec],
        out_specs=[pl.BlockSpec((tm, d), lambda i: (i, 0)),
                   pl.BlockSpec((tm * sub, LANES), lambda i: (i, 0))],
        out_shape=[jax.ShapeDtypeStruct((t, d), _F32),
                   jax.ShapeDtypeStruct((t * sub, LANES), jnp.uint32)],
        compiler_params=_params(("parallel",)),
        name="ln_mod",
    )(y, g, b, scale, shift)


def _first_index(hit, idx, sentinel, axes):
    v = jnp.where(hit, idx, sentinel)
    for ax in axes:
        v = jnp.min(v, axis=ax, keepdims=True)
    return v


def _max_over(v, axes):
    for ax in axes:
        v = jnp.max(v, axis=ax, keepdims=True)
    return v


def _sum_over(v, axes):
    for ax in axes:
        v = jnp.sum(v, axis=ax, keepdims=True)
    return v


def _route_kernel(hp_ref, wr_ref, bias_ref, tri_ref, idx_ref, wgt_ref, rank_ref, ccol_ref, crow_ref,
                  carry_ref, *, n_exp, sub):
    i = pl.program_id(0)
    per_group = n_exp // N_GROUPS

    @pl.when(i == 0)
    def _():
        carry_ref[...] = jnp.zeros_like(carry_ref)
        crow_ref[...] = jnp.zeros_like(crow_ref)

    tm = hp_ref.shape[0] // sub
    h = _unpack_bf16_pairs(_load_token_rows(hp_ref, tm, sub)).astype(_BF16)
    logits = lax.dot_general(wr_ref[...], h, (((1,), (1,)), ((), ())),
                             preferred_element_type=_F32)
    scores = jax.nn.sigmoid(logits)
    biased = scores + bias_ref[...]
    shape3 = (N_GROUPS, per_group, tm)
    s3 = scores.reshape(shape3)
    b3 = biased.reshape(shape3)
    gi = lax.broadcasted_iota(jnp.int32, shape3, 0).astype(_F32)
    wi = lax.broadcasted_iota(jnp.int32, shape3, 1).astype(_F32)
    ei = gi * per_group + wi
    neg = -jnp.inf

    m1 = jnp.max(b3, axis=1, keepdims=True)
    f1 = _first_index(b3 == m1, wi, float(per_group), (1,))
    m2 = jnp.max(jnp.where(wi == f1, neg, b3), axis=1, keepdims=True)
    gs = m1 + m2
    g1 = lax.broadcasted_iota(jnp.int32, gs.shape, 0).astype(_F32)
    gmask = jnp.zeros(gs.shape, _F32)
    for _ in range(TOPK_GROUPS):
        gm = jnp.max(gs, axis=0, keepdims=True)
        gf = _first_index(gs == gm, g1, float(N_GROUPS), (0,))
        sel = g1 == gf
        gmask = jnp.where(sel, 1.0, gmask)
        gs = jnp.where(sel, neg, gs)
    masked = jnp.where(jnp.broadcast_to(gmask, shape3) > 0.0, b3, neg)

    chosen = jnp.zeros(shape3, _F32)
    idxs, wgts = [], []
    for _ in range(TOP_K):
        m = _max_over(masked, (0, 1))
        f = _first_index(masked == m, ei, float(n_exp), (0, 1))
        sel = ei == f
        wgts.append(_sum_over(jnp.where(sel, s3, 0.0), (0, 1)))
        idxs.append(f)
        chosen = jnp.where(sel, 1.0, chosen)
        masked = jnp.where(sel, neg, masked)
    wsum = wgts[0]
    for w in wgts[1:]:
        wsum = wsum + w

    chosen2 = chosen.reshape(n_exp, tm)
    before = jnp.dot(chosen2.astype(_BF16), tri_ref[...], preferred_element_type=_F32) + carry_ref[...]
    before3 = before.reshape(shape3)
    for k in range(TOP_K):
        sel = ei == idxs[k]
        rank = _sum_over(jnp.where(sel, before3, 0.0), (0, 1))
        idx_ref[k:k + 1, :] = idxs[k].reshape(1, tm).astype(jnp.int32)
        wgt_ref[k:k + 1, :] = (wgts[k] / wsum * ROUTE_SCALE).reshape(1, tm)
        rank_ref[k:k + 1, :] = rank.reshape(1, tm).astype(jnp.int32)
    pad = TOP_K_PAD - TOP_K
    idx_ref[TOP_K:, :] = jnp.zeros((pad, tm), jnp.int32)
    wgt_ref[TOP_K:, :] = jnp.zeros((pad, tm), _F32)
    rank_ref[TOP_K:, :] = jnp.zeros((pad, tm), jnp.int32)

    carry_ref[...] = carry_ref[...] + jnp.sum(chosen2, axis=1, keepdims=True)
    ccol_ref[...] = carry_ref[...]
    crow_ref[...] += lax.dot_general(jnp.ones((SUBLANES, tm), _BF16), chosen2.astype(_BF16),
                                     (((1,), (1,)), ((), ())), preferred_element_type=_F32)


def _route(hp, wr_t, bias_col, tm=512):
    n_exp, d = wr_t.shape
    sub = d // 2 // LANES
    t = hp.shape[0] // sub
    tm = min(tm, t)
    tri = (lax.broadcasted_iota(jnp.int32, (tm, tm), 0)
           < lax.broadcasted_iota(jnp.int32, (tm, tm), 1)).astype(_BF16)
    tok = lambda dt: jax.ShapeDtypeStruct((TOP_K_PAD, t), dt)
    tok_spec = pl.BlockSpec((TOP_K_PAD, tm), lambda i: (0, i))
    return pl.pallas_call(
        functools.partial(_route_kernel, n_exp=n_exp, sub=sub),
        grid=(t // tm,),
        in_specs=[pl.BlockSpec((tm * sub, LANES), lambda i: (i, 0)),
                  pl.BlockSpec((n_exp, d), lambda i: (0, 0)),
                  pl.BlockSpec((n_exp, 1), lambda i: (0, 0)),
                  pl.BlockSpec((tm, tm), lambda i: (0, 0))],
        out_specs=[tok_spec, tok_spec, tok_spec,
                   pl.BlockSpec((n_exp, 1), lambda i: (0, 0)),
                   pl.BlockSpec((SUBLANES, n_exp), lambda i: (0, 0))],
        out_shape=[tok(jnp.int32), tok(_F32), tok(jnp.int32),
                   jax.ShapeDtypeStruct((n_exp, 1), _F32),
                   jax.ShapeDtypeStruct((SUBLANES, n_exp), _F32)],
        scratch_shapes=[pltpu.VMEM((n_exp, 1), _F32)],
        compiler_params=_params(("arbitrary",)),
        name="route",
    )(hp, wr_t, bias_col, tri)


def _plan_kernel(idx_ref, rank_ref, ccol_ref, crow_ref, dest_ref, sched_ref, *, n_exp):
    bm = MOE_ROW_BLOCK
    ccol = ccol_ref[...]
    crow = crow_ref[0:1, :]
    pad_col = jnp.ceil(ccol / bm) * bm
    pad_row = jnp.ceil(crow / bm) * bm
    er = lax.broadcasted_iota(jnp.int32, (n_exp, n_exp), 0)
    ec = lax.broadcasted_iota(jnp.int32, (n_exp, n_exp), 1)
    pend_col = jnp.sum(jnp.where(ec <= er, jnp.broadcast_to(pad_row, (n_exp, n_exp)), 0.0),
                       axis=1, keepdims=True)
    pstart_col = pend_col - pad_col

    t = idx_ref.shape[1]
    e_tok = lax.broadcasted_iota(jnp.int32, (n_exp, t), 0)
    for k in range(TOP_K_PAD):
        hit = e_tok == idx_ref[k:k + 1, :]
        start = jnp.sum(jnp.where(hit, pstart_col, 0.0), axis=0, keepdims=True)
        dest_ref[k:k + 1, :] = start.astype(jnp.int32) + rank_ref[k:k + 1, :]

    nb = sched_ref.shape[1]
    blk = lax.broadcasted_iota(jnp.int32, (n_exp, nb), 1).astype(_F32) * bm
    e_blk = lax.broadcasted_iota(jnp.int32, (n_exp, nb), 0)
    block_e = jnp.sum(jnp.where(pend_col <= blk, 1.0, 0.0), axis=0, keepdims=True)
    block_e = jnp.minimum(block_e, n_exp - 1.0)
    is_first = jnp.max(jnp.where(jnp.logical_and(pstart_col == blk, pad_col > 0.0), 1.0, 0.0),
                       axis=0, keepdims=True)
    later = jnp.logical_and(e_blk.astype(_F32) > block_e, pad_col > 0.0)
    nxt = jnp.min(jnp.where(later, e_blk.astype(_F32), float(n_exp)), axis=0, keepdims=True)
    total = jnp.max(pend_col, axis=0, keepdims=True)
    n_used = jnp.broadcast_to(total / bm, (1, nb))
    on_diag = e_blk == lax.broadcasted_iota(jnp.int32, (n_exp, nb), 1)
    seg_end = jnp.sum(jnp.where(on_diag, pstart_col + ccol, 0.0), axis=0, keepdims=True)
    pad_len = jnp.sum(jnp.where(on_diag, pad_col - ccol, 0.0), axis=0, keepdims=True)
    sched_ref[0:1, :] = block_e.astype(jnp.int32)
    sched_ref[1:2, :] = is_first.astype(jnp.int32)
    sched_ref[2:3, :] = nxt.astype(jnp.int32)
    sched_ref[3:4, :] = n_used.astype(jnp.int32)
    sched_ref[4:5, :] = seg_end.astype(jnp.int32)
    sched_ref[5:6, :] = pad_len.astype(jnp.int32)
    sched_ref[6:, :] = jnp.zeros((SUBLANES - 6, nb), jnp.int32)


def _plan(idx, rank, ccol, crow, n_blocks):
    n_exp = ccol.shape[0]
    t = idx.shape[1]
    nb = max(pl.cdiv(n_blocks, LANES) * LANES, LANES)
    assert nb >= n_exp
    return pl.pallas_call(
        functools.partial(_plan_kernel, n_exp=n_exp),
        out_shape=[jax.ShapeDtypeStruct((TOP_K_PAD, t), jnp.int32),
                   jax.ShapeDtypeStruct((SUBLANES, nb), jnp.int32)],
        compiler_params=pltpu.CompilerParams(vmem_limit_bytes=VMEM_LIMIT),
        name="plan",
    )(idx, rank, ccol, crow)


def _dispatch_kernel(sched_ref, dest_ref, hp_ref, xs_ref, zero_ref, sem, zsem, *, n_exp, n_row_blocks, tb, sub):
    i = pl.program_id(0)
    bm = MOE_ROW_BLOCK

    def token_rows(tok, n=1):
        return pl.ds(pl.multiple_of(tok * sub, sub), n * sub)

    def zero_copy(row0, rows):
        return pltpu.make_async_copy(zero_ref.at[pl.ds(0, rows * sub)], xs_ref.at[token_rows(row0, rows)], zsem)

    def pad_pieces(e, fn):
        end, pad = sched_ref[4, e], sched_ref[5, e]
        size = bm // 2
        while size >= 1:
            @pl.when((pad & size) != 0)
            def _(size=size):
                fn(zero_copy(end + (pad & ~(2 * size - 1)), size))
            size //= 2

    @pl.when(i == 0)
    def _():
        zero_ref[...] = jnp.zeros_like(zero_ref)
        n_used = sched_ref[3, 0]

        def start_seg(e, c):
            pad_pieces(e, lambda cp: cp.start())
            return c

        def wait_seg(e, c):
            pad_pieces(e, lambda cp: cp.wait())
            return c

        def start_blk(b, c):
            zero_copy(b * bm, bm).start()
            return c

        def wait_blk(b, c):
            zero_copy(b * bm, bm).wait()
            return c

        lax.fori_loop(0, n_exp, start_seg, 0)
        lax.fori_loop(n_used, n_row_blocks, start_blk, 0)
        lax.fori_loop(0, n_exp, wait_seg, 0)
        lax.fori_loop(n_used, n_row_blocks, wait_blk, 0)

    def row_copy(r, k):
        tok = i * tb + r
        return pltpu.make_async_copy(hp_ref.at[token_rows(tok)], xs_ref.at[token_rows(dest_ref[k, r])], sem)

    def start(r, c):
        for k in range(TOP_K):
            row_copy(r, k).start()
        return c

    def wait(r, c):
        for k in range(TOP_K):
            row_copy(r, k).wait()
        return c

    lax.fori_loop(0, tb, start, 0)
    lax.fori_loop(0, tb, wait, 0)


def _dispatch(sched, dest, hp, n_exp, n_row_blocks, tb=512):
    t = dest.shape[1]
    sub = hp.shape[0] // t
    tb = min(tb, t)
    n_rows = n_row_blocks * MOE_ROW_BLOCK
    grid_spec = pltpu.PrefetchScalarGridSpec(
        num_scalar_prefetch=1,
        grid=(t // tb,),
        in_specs=[pl.BlockSpec((TOP_K_PAD, tb), lambda i, s: (0, i), memory_space=pltpu.SMEM),
                  pl.BlockSpec(memory_space=pl.ANY)],
        out_specs=pl.BlockSpec(memory_space=pl.ANY),
        scratch_shapes=[pltpu.VMEM((MOE_ROW_BLOCK * sub, LANES), jnp.uint32),
                        pltpu.SemaphoreType.DMA, pltpu.SemaphoreType.DMA],
    )
    return pl.pallas_call(
        functools.partial(_dispatch_kernel, n_exp=n_exp, n_row_blocks=n_row_blocks, tb=tb, sub=sub),
        grid_spec=grid_spec,
        out_shape=jax.ShapeDtypeStruct((n_rows * sub, LANES), jnp.uint32),
        compiler_params=_params(("arbitrary",)),
        name="dispatch",
    )(sched, dest, hp)


def _experts_kernel(sched_ref, xs_ref, wg_hbm, wu_hbm, wd_hbm, y_ref,
                    wg_f, wu_f, wd_f, wg_b, wu_b, wd_b, sems, *, sub):
    b = pl.program_id(0)
    bm = MOE_ROW_BLOCK
    e = sched_ref[0, b]
    n_used = sched_ref[3, 0]

    def weight_copies(expert):
        return (pltpu.make_async_copy(wg_hbm.at[expert], wg_f, sems.at[0]),
                pltpu.make_async_copy(wu_hbm.at[expert], wu_f, sems.at[1]),
                pltpu.make_async_copy(wd_hbm.at[expert], wd_f, sems.at[2]))

    @pl.when(b == 0)
    def _():
        for cp in weight_copies(e):
            cp.start()

    @pl.when(jnp.logical_and(b < n_used, sched_ref[1, b] == 1))
    def _():
        for cp in weight_copies(e):
            cp.wait()
        wg_b[...] = wg_f[...].astype(_BF16)
        wu_b[...] = wu_f[...].astype(_BF16)
        wd_b[...] = wd_f[...].astype(_BF16)
        nxt = sched_ref[2, b]

        @pl.when(nxt < wg_hbm.shape[0])
        def _():
            for cp in weight_copies(nxt):
                cp.start()

    @pl.when(b < n_used)
    def _():
        x = _unpack_bf16_pairs(_load_token_rows(xs_ref, bm, sub)).astype(_BF16)
        hid = (_silu(jnp.dot(x, wg_b[...], preferred_element_type=_F32))
               * jnp.dot(x, wu_b[...], preferred_element_type=_F32))
        y = jnp.dot(hid.astype(_BF16), wd_b[...], preferred_element_type=_F32)
        _store_token_rows(y_ref, _pack_bf16_pairs(y), sub)

    @pl.when(b >= n_used)
    def _():
        y_ref[...] = jnp.zeros_like(y_ref)


def _experts(sched, xs, w_gate, w_up, w_down, n_blocks):
    n_exp, d, f = w_gate.shape
    sub = d // 2 // LANES
    bm = MOE_ROW_BLOCK
    grid_spec = pltpu.PrefetchScalarGridSpec(
        num_scalar_prefetch=1,
        grid=(n_blocks,),
        in_specs=[pl.BlockSpec((bm * sub, LANES), lambda b, s: (jnp.minimum(b, s[3, 0] - 1), 0)),
                  pl.BlockSpec(memory_space=pl.ANY),
                  pl.BlockSpec(memory_space=pl.ANY),
                  pl.BlockSpec(memory_space=pl.ANY)],
        out_specs=pl.BlockSpec((bm * sub, LANES), lambda b, s: (b, 0)),
        scratch_shapes=[pltpu.VMEM((d, f), _F32), pltpu.VMEM((d, f), _F32), pltpu.VMEM((f, d), _F32),
                        pltpu.VMEM((d, f), _BF16), pltpu.VMEM((d, f), _BF16), pltpu.VMEM((f, d), _BF16),
                        pltpu.SemaphoreType.DMA((3,))],
    )
    return pl.pallas_call(
        functools.partial(_experts_kernel, sub=sub),
        grid_spec=grid_spec,
        out_shape=jax.ShapeDtypeStruct((n_blocks * bm * sub, LANES), jnp.uint32),
        compiler_params=_params(("arbitrary",)),
        name="experts",
    )(sched, xs, w_gate, w_up, w_down)


def _shared_kernel(hp_ref, wg_ref, wu_ref, wd_ref, o_ref, *, sub):
    x = _unpack_bf16_pairs(_load_token_rows(hp_ref, o_ref.shape[0], sub)).astype(_BF16)
    hid = (_silu(jnp.dot(x, wg_ref[...], preferred_element_type=_F32))
           * jnp.dot(x, wu_ref[...], preferred_element_type=_F32))
    o_ref[...] = jnp.dot(hid.astype(_BF16), wd_ref[...], preferred_element_type=_F32)


def _shared(hp, wg, wu, wd, tm=512):
    d, f = wg.shape
    sub = d // 2 // LANES
    t = hp.shape[0] // sub
    tm = min(tm, t)
    return pl.pallas_call(
        functools.partial(_shared_kernel, sub=sub),
        grid=(t // tm,),
        in_specs=[pl.BlockSpec((tm * sub, LANES), lambda i: (i, 0)),
                  pl.BlockSpec((d, f), lambda i: (0, 0)),
                  pl.BlockSpec((d, f), lambda i: (0, 0)),
                  pl.BlockSpec((f, d), lambda i: (0, 0))],
        out_specs=pl.BlockSpec((tm, d), lambda i: (i, 0)),
        out_shape=jax.ShapeDtypeStruct((t, d), _F32),
        compiler_params=_params(("parallel",)),
        name="shared",
    )(hp, wg, wu, wd)


def _combine_kernel(dest_ref, ys_hbm, wgt_ref, sh_ref, x1_ref, gate_ref, g_ref, b_ref, o_ref,
                    buf, sem, *, alpha, tb, sub):
    def row_copy(r, k):
        src = ys_hbm.at[pl.ds(pl.multiple_of(dest_ref[k, r] * sub, sub), sub)]
        return pltpu.make_async_copy(src, buf.at[k, pl.ds(pl.multiple_of(r * sub, sub), sub)], sem)

    def start(r, c):
        for k in range(TOP_K):
            row_copy(r, k).start()
        return c

    def wait(r, c):
        for k in range(TOP_K):
            row_copy(r, k).wait()
        return c

    lax.fori_loop(0, tb, start, 0)
    lax.fori_loop(0, tb, wait, 0)

    ffn = sh_ref[...]
    for k in range(TOP_K):
        ffn = ffn + wgt_ref[:, k:k + 1] * _unpack_bf16_pairs(_load_token_rows(buf.at[k], tb, sub))
    y = alpha * x1_ref[...] + gate_ref[...] * ffn
    o_ref[...] = _layer_norm_rows(y, g_ref[...], b_ref[...])


def _combine(dest, ys, wgt_tk, shared, x1, gate, g, b, alpha, tb=256):
    t, d = x1.shape
    sub = d // 2 // LANES
    tb = min(tb, t)
    vec = pl.BlockSpec((1, d), lambda i: (0, 0))
    return pl.pallas_call(
        functools.partial(_combine_kernel, alpha=alpha, tb=tb, sub=sub),
        grid=(t // tb,),
        in_specs=[pl.BlockSpec((TOP_K_PAD, tb), lambda i: (0, i), memory_space=pltpu.SMEM),
                  pl.BlockSpec(memory_space=pl.ANY),
                  pl.BlockSpec((tb, TOP_K_PAD), lambda i: (i, 0)),
                  pl.BlockSpec((tb, d), lambda i: (i, 0)),
                  pl.BlockSpec((tb, d), lambda i: (i, 0)),
                  vec, vec, vec],
        out_specs=pl.BlockSpec((tb, d), lambda i: (i, 0)),
        out_shape=jax.ShapeDtypeStruct((t, d), _F32),
        scratch_shapes=[pltpu.VMEM((TOP_K, tb * sub, LANES), jnp.uint32), pltpu.SemaphoreType.DMA],
        compiler_params=_params(("arbitrary",)),
        name="combine",
    )(dest, ys, wgt_tk, shared, x1, gate, g, b)


def kernel(x, c, w_ada, b_ada, w_in, b_forget, w_dw, b_dw, conv_norm_g, conv_norm_b, w_out, ln1_g, ln1_b, w_router, router_bias, w_exp_gate, w_exp_up, w_exp_down, w_sh_gate, w_sh_up, w_sh_down, ln2_g, ln2_b):
    bsz, seq, d = x.shape
    depth = w_ada.shape[0]
    heads = b_forget.shape[-1]
    attn_w = heads * HEAD_DIM
    conv_w = w_dw.shape[-1]
    n_exp = w_router.shape[-1]
    alpha = (2.0 * depth) ** 0.25
    t = bsz * seq
    assert bsz == 1, "the sequence axis is tiled directly; one sequence per call"
    n_blocks = (t * TOP_K + n_exp * (MOE_ROW_BLOCK - 1) + MOE_ROW_BLOCK - 1) // MOE_ROW_BLOCK
    row = lambda v: v.reshape(1, -1)

    xt = x.reshape(t, d)
    for l in range(depth):
        mod = _adaln(c.reshape(d, 1), w_ada[l], row(b_ada[l]))
        shift_a, scale_a, gate_a, shift_f, scale_f, gate_f = (mod[:, k * d:(k + 1) * d] for k in range(6))

        w = w_in[l]
        qk_scale = HEAD_DIM ** -0.5
        w_qkv = jnp.concatenate([w[:, :attn_w] * qk_scale, w[:, attn_w:3 * attn_w]], axis=1).astype(_BF16)
        w_f = jnp.pad(w[:, 3 * attn_w:3 * attn_w + heads], ((0, 0), (0, LANES - heads))).astype(_BF16)
        w_c = w[:, 3 * attn_w + heads:].astype(_BF16)
        h = _modulate(xt, scale_a, shift_a)
        qkv = _matmul(h, w_qkv, _BF16, name="proj_qkv")
        conv_in = _matmul(h, w_c, _F32, name="proj_conv")
        f_logit = _matmul(h, w_f, _F32, tn=LANES, name="proj_forget")
        b_pad = jnp.pad(row(b_forget[l]), ((0, 0), (0, LANES - heads)))
        qx, kx = _forget_cum(f_logit, b_pad, heads)
        attn = _attention(qkv, qx, kx, heads)
        conv = _conv(conv_in, w_dw[l], row(b_dw[l]), row(conv_norm_g[l]), row(conv_norm_b[l]))
        y1 = _outproj(attn, conv, w_out[l].astype(_BF16), xt, gate_a, alpha)
        x1, hp = _ln_mod(y1, row(ln1_g[l]), row(ln1_b[l]), scale_f, shift_f)

        idx, wgt, rank, ccol, crow = _route(hp, w_router[l].T.astype(_BF16), router_bias[l].reshape(n_exp, 1))
        dest, sched = _plan(idx, rank, ccol, crow, n_blocks)
        xs = _dispatch(sched, dest, hp, n_exp, n_blocks)
        ys = _experts(sched, xs, w_exp_gate[l], w_exp_up[l], w_exp_down[l], n_blocks)
        shared = _shared(hp, w_sh_gate[l].astype(_BF16), w_sh_up[l].astype(_BF16), w_sh_down[l].astype(_BF16))
        xt = _combine(dest, ys, wgt.T, shared, x1, gate_f, row(ln2_g[l]), row(ln2_b[l]), alpha)
    return xt.reshape(bsz, seq, d)
```
